```python
import math
import jax, jax.numpy as jnp
from jax import lax
import numpy as np

D_MODEL = 2048
BATCH = 2
SEQ = 16384
DEPTH = 1

CHUNK = 64
N_META = 16
EPS = 1e-6
D_SC = D_MODEL
SC_KERNEL = 3
MB_EXPAND = 2
D_INNER = MB_EXPAND * D_MODEL
MB_HEADDIM = 64
MB_HEADS = D_INNER // MB_HEADDIM
MB_GROUPS = 8
MB_STATE = 128
MB_CONV = 4
D_BC = MB_GROUPS * MB_STATE
D_XBC = D_INNER + 2 * D_BC
N_BRANCH = 2
D_FF = 5504
FFN_KERNEL = 3
IN_SIZES = (D_SC, D_SC, D_SC, D_INNER, D_XBC, MB_HEADS, N_BRANCH * D_MODEL)
D_IN_PROJ = 3 * D_SC + D_INNER + D_XBC + MB_HEADS + N_BRANCH * D_MODEL

kernel_name = "hybrid_shortconv_ssd_gated_block"


def rmsnorm(x, g):
    x32 = x.astype(jnp.float32)
    y = x32 * lax.rsqrt(jnp.mean(x32 * x32, axis=-1, keepdims=True) + EPS)
    return (y * g.astype(jnp.float32)).astype(x.dtype)


def causal_dwconv(x, w, b=None):
    K = w.shape[0]
    L = x.shape[1]
    xp = jnp.pad(x, ((0, 0), (K - 1, 0), (0, 0)))
    y = xp[:, K - 1:K - 1 + L] * w[K - 1]
    for k in range(K - 1):
        y = y + xp[:, k:k + L] * w[k]
    if b is not None:
        y = y + b
    return y


def split_cols(a, sizes):
    idx, run = [], 0
    for s in sizes[:-1]:
        run += s
        idx.append(run)
    return jnp.split(a, idx, axis=-1)


def ssd_chunked(xs, dt, A, Bm, Cm):
    b, T, H, P = xs.shape
    G, N = Bm.shape[2], Bm.shape[3]
    R = H // G
    nc = T // CHUNK
    f32 = jnp.float32
    x = xs.astype(f32).reshape(b, nc, CHUNK, G, R, P)
    dtc = dt.astype(f32).reshape(b, nc, CHUNK, G, R)
    Bc = Bm.astype(f32).reshape(b, nc, CHUNK, G, N)
    Cc = Cm.astype(f32).reshape(b, nc, CHUNK, G, N)
    acum = jnp.cumsum(dtc * A.astype(f32).reshape(G, R), axis=2)
    xdt = x * dtc[..., None]
    acum_t = jnp.moveaxis(acum, 2, -1)
    seg = acum_t[..., :, None] - acum_t[..., None, :]
    causal = jnp.tril(jnp.ones((CHUNK, CHUNK), dtype=bool))
    decay = jnp.exp(jnp.where(causal, seg, -jnp.inf))
    cb = jnp.einsum('bclgn,bcsgn->bcgls', Cc, Bc)
    y_diag = jnp.einsum('bcgrls,bcsgrp->bclgrp', decay * cb[:, :, :, None], xdt)
    a_last = acum[:, :, -1:]
    state_in = xdt * jnp.exp(a_last - acum)[..., None]
    chunk_decay = jnp.exp(a_last[:, :, 0])
    out_decay = jnp.exp(acum)

    def step(state, inp):
        Bk, Ck, sik, cdk, odk = inp
        y_off = jnp.einsum('blgn,bgrpn->blgrp', Ck, state) * odk[..., None]
        state = state * cdk[..., None, None] + jnp.einsum('blgn,blgrp->bgrpn', Bk, sik)
        return state, y_off

    to_c = lambda t: jnp.moveaxis(t, 1, 0)
    state0 = jnp.zeros((b, G, R, P, N), f32)
    _, y_off = lax.scan(step, state0, (to_c(Bc), to_c(Cc), to_c(state_in), to_c(chunk_decay), to_c(out_decay)))
    y = y_diag + jnp.moveaxis(y_off, 0, 1)
    return y.reshape(b, T, H, P).astype(xs.dtype)


def hybrid_layer(h, norm1_g, w_in, b_gate, sc_conv_w, mb_conv_w, mb_conv_b, dt_bias, a_log,
                 d_skip, mb_norm_g, w_a, w_m, w_o, norm2_g, w_up, ffn_conv_w, ffn_conv_b, w_down):
    bsz, L, _ = h.shape
    xn = rmsnorm(h, norm1_g)
    proj = xn @ w_in
    sc_b, sc_c, sc_h, z, xbc, dt_raw, gate_raw = split_cols(proj, IN_SIZES)

    y_a = sc_b * causal_dwconv(sc_c * sc_h, sc_conv_w)

    xbc = jax.nn.silu(causal_dwconv(xbc, mb_conv_w, mb_conv_b))
    xs, Bm, Cm = jnp.split(xbc, [D_INNER, D_INNER + D_BC], axis=-1)
    xs = xs.reshape(bsz, L, MB_HEADS, MB_HEADDIM)
    dt = jax.nn.softplus(dt_raw.astype(jnp.float32) + dt_bias.astype(jnp.float32))
    A = -jnp.exp(a_log.astype(jnp.float32))
    pad_left = (-L) % CHUNK
    padseq = lambda a: jnp.pad(a, [(0, 0), (pad_left, 0)] + [(0, 0)] * (a.ndim - 2))
    y_ssd = ssd_chunked(padseq(xs), padseq(dt), A,
                        padseq(Bm.reshape(bsz, L, MB_GROUPS, MB_STATE)),
                        padseq(Cm.reshape(bsz, L, MB_GROUPS, MB_STATE)))[:, pad_left:]
    y_m = (y_ssd + xs * d_skip[:, None].astype(xs.dtype)).reshape(bsz, L, D_INNER)
    y_m = y_m * jax.nn.silu(z)
    y_m = rmsnorm(y_m.reshape(bsz, L, MB_GROUPS, D_INNER // MB_GROUPS),
                  mb_norm_g.reshape(MB_GROUPS, D_INNER // MB_GROUPS)).reshape(bsz, L, D_INNER)

    g_a, g_m = jnp.split(jax.nn.sigmoid(gate_raw + b_gate), N_BRANCH, axis=-1)
    mix = g_a * (y_a @ w_a) + g_m * (y_m @ w_m)
    h = h + mix @ w_o

    xn2 = rmsnorm(h, norm2_g)
    u, v = jnp.split(xn2 @ w_up, 2, axis=-1)
    u = causal_dwconv(u, ffn_conv_w, ffn_conv_b)
    h = h + (jax.nn.silu(u) * v) @ w_down
    return h


def setup_inputs(seed: int = 0) -> dict:
    key = jax.random.key(seed)
    ks = jax.random.split(key, 24)
    f32 = jnp.float32
    nrm = lambda k, shape, s: jax.random.normal(k, shape, f32) * s
    x = nrm(ks[0], (BATCH, SEQ, D_MODEL), 1.0)
    meta_tokens = nrm(ks[1], (N_META, D_MODEL), 1.0)
    norm1_g = 1.0 + nrm(ks[2], (DEPTH, D_MODEL), 0.05)
    w_in = nrm(ks[3], (DEPTH, D_MODEL, D_IN_PROJ), D_MODEL ** -0.5)
    b_gate = nrm(ks[4], (DEPTH, N_BRANCH * D_MODEL), 0.1)
    sc_conv_w = nrm(ks[5], (DEPTH, SC_KERNEL, D_SC), SC_KERNEL ** -0.5)
    mb_conv_w = nrm(ks[6], (DEPTH, MB_CONV, D_XBC), MB_CONV ** -0.5)
    mb_conv_b = nrm(ks[7], (DEPTH, D_XBC), 0.01)
    dt0 = jnp.exp(jax.random.uniform(ks[8], (DEPTH, MB_HEADS), f32)
                  * (math.log(0.1) - math.log(0.001)) + math.log(0.001))
    dt_bias = dt0 + jnp.log(-jnp.expm1(-dt0))
    a_log = jnp.log(jax.random.uniform(ks[9], (DEPTH, MB_HEADS), f32, 1.0, 16.0))
    d_skip = 1.0 + nrm(ks[10], (DEPTH, MB_HEADS), 0.1)
    mb_norm_g = 1.0 + nrm(ks[11], (DEPTH, D_INNER), 0.05)
    w_a = nrm(ks[12], (DEPTH, D_SC, D_MODEL), D_SC ** -0.5)
    w_m = nrm(ks[13], (DEPTH, D_INNER, D_MODEL), D_INNER ** -0.5)
    w_o = nrm(ks[14], (DEPTH, D_MODEL, D_MODEL), D_MODEL ** -0.5)
    norm2_g = 1.0 + nrm(ks[15], (DEPTH, D_MODEL), 0.05)
    w_up = nrm(ks[16], (DEPTH, D_MODEL, 2 * D_FF), D_MODEL ** -0.5)
    ffn_conv_w = nrm(ks[17], (DEPTH, FFN_KERNEL, D_FF), FFN_KERNEL ** -0.5)
    ffn_conv_b = nrm(ks[18], (DEPTH, D_FF), 0.01)
    w_down = nrm(ks[19], (DEPTH, D_FF, D_MODEL), D_FF ** -0.5)
    normf_g = 1.0 + nrm(ks[20], (D_MODEL,), 0.05)
    return {"x": x, "meta_tokens": meta_tokens, "norm1_g": norm1_g, "w_in": w_in,
            "b_gate": b_gate, "sc_conv_w": sc_conv_w, "mb_conv_w": mb_conv_w,
            "mb_conv_b": mb_conv_b, "dt_bias": dt_bias, "a_log": a_log, "d_skip": d_skip,
            "mb_norm_g": mb_norm_g, "w_a": w_a, "w_m": w_m, "w_o": w_o, "norm2_g": norm2_g,
            "w_up": w_up, "ffn_conv_w": ffn_conv_w, "ffn_conv_b": ffn_conv_b,
            "w_down": w_down, "normf_g": normf_g}


def reference(x, meta_tokens, norm1_g, w_in, b_gate, sc_conv_w, mb_conv_w, mb_conv_b, dt_bias,
              a_log, d_skip, mb_norm_g, w_a, w_m, w_o, norm2_g, w_up, ffn_conv_w, ffn_conv_b,
              w_down, normf_g):
    bsz = x.shape[0]
    meta = jnp.broadcast_to(meta_tokens[None].astype(x.dtype), (bsz, N_META, D_MODEL))
    h = jnp.concatenate([meta, x], axis=1)
    for i in range(DEPTH):
        h = hybrid_layer(h, norm1_g[i], w_in[i], b_gate[i], sc_conv_w[i], mb_conv_w[i],
                         mb_conv_b[i], dt_bias[i], a_log[i], d_skip[i], mb_norm_g[i], w_a[i],
                         w_m[i], w_o[i], norm2_g[i], w_up[i], ffn_conv_w[i], ffn_conv_b[i],
                         w_down[i])
    return rmsnorm(h[:, N_META:], normf_g)
```

```python
import functools

import jax
import jax.numpy as jnp
from jax import lax
from jax.experimental import pallas as pl
from jax.experimental.pallas import tpu as pltpu

EPS = 1e-6
SSD_STATE = 128
SSD_CHUNK = 128
LANES = 128
HALO = 8
VMEM_LIMIT_BYTES = 56 * 1024 * 1024

F32 = jnp.float32
BF16 = jnp.bfloat16


def _cparams(n_axes):
    return pltpu.CompilerParams(dimension_semantics=("arbitrary",) * n_axes,
                                vmem_limit_bytes=VMEM_LIMIT_BYTES)


def _tile(n, pref):
    if n <= pref:
        return n
    t = (pref // LANES) * LANES
    while n % t:
        t -= LANES
    return t


def _dot(a, b):
    return jnp.dot(a, b, preferred_element_type=F32)


def _dot_nt(a, b):
    return lax.dot_general(a, b, (((1,), (1,)), ((), ())), preferred_element_type=F32)


def _dot_tn(a, b):
    return lax.dot_general(a, b, (((0,), (0,)), ((), ())), preferred_element_type=F32)


def _silu(x):
    return x / (1.0 + jnp.exp(-x))


def _split3(a):
    hi = a.astype(BF16)
    r = a - hi.astype(F32)
    mid = r.astype(BF16)
    lo = (r - mid.astype(F32)).astype(BF16)
    return hi, mid, lo


def _rmsnorm_body(h_ref, g_ref, o_ref):
    x = h_ref[0]
    ms = jnp.mean(x * x, axis=-1, keepdims=True)
    o_ref[0] = (x * lax.rsqrt(ms + EPS) * g_ref[...]).astype(o_ref.dtype)


def _rmsnorm_cast(h, g, tm):
    B, L, D = h.shape
    return pl.pallas_call(
        _rmsnorm_body,
        grid=(B, L // tm),
        in_specs=[pl.BlockSpec((1, tm, D), lambda b, i: (b, i, 0)),
                  pl.BlockSpec((1, D), lambda b, i: (0, 0))],
        out_specs=pl.BlockSpec((1, tm, D), lambda b, i: (b, i, 0)),
        out_shape=jax.ShapeDtypeStruct((B, L, D), BF16),
        compiler_params=_cparams(2),
        name="rmsnorm_cast",
    )(h, g.reshape(1, D).astype(F32))


def _proj_pointwise_body(x_ref, w_ref, b_ref, o_ref, *, act, n_pad, tm):
    r = _dot(x_ref[0], w_ref[...]) + b_ref[...]
    if act == "silu":
        r = _silu(r)
    elif act == "sigmoid":
        r = 1.0 / (1.0 + jnp.exp(-r))
    elif act == "softplus":
        r = jnp.maximum(r, 0.0) + jnp.log1p(jnp.exp(-jnp.abs(r)))
    if n_pad:
        row = pl.program_id(2) * tm + lax.broadcasted_iota(jnp.int32, r.shape, 0)
        r = jnp.where(row >= n_pad, r, 0.0)
    o_ref[0] = r.astype(o_ref.dtype)


def _proj_pointwise(x, w, bias, *, act, out_dtype, tm, tn, n_pad=0, name):
    B, L, D = x.shape
    N = w.shape[1]
    body = functools.partial(_proj_pointwise_body, act=act, n_pad=n_pad, tm=tm)
    return pl.pallas_call(
        body,
        grid=(N // tn, B, L // tm),
        in_specs=[pl.BlockSpec((1, tm, D), lambda j, b, i: (b, i, 0)),
                  pl.BlockSpec((D, tn), lambda j, b, i: (0, j)),
                  pl.BlockSpec((1, tn), lambda j, b, i: (0, j))],
        out_specs=pl.BlockSpec((1, tm, tn), lambda j, b, i: (b, i, j)),
        out_shape=jax.ShapeDtypeStruct((B, L, N), out_dtype),
        compiler_params=_cparams(3),
        name=name,
    )(x, w, bias)


def _proj_conv_body(*refs, K, has_p2, has_gate, has_bias, act, tm):
    refs = list(refs)
    x_ref = refs.pop(0)
    wp_ref = refs.pop(0)
    wp2_ref = refs.pop(0) if has_p2 else None
    wg_ref = refs.pop(0) if has_gate else None
    cw_ref = refs.pop(0)
    cb_ref = refs.pop(0) if has_bias else None
    halo_ref, o_ref, tail_ref, buf = refs

    @pl.when(pl.program_id(2) == 0)
    def _():
        buf[0:HALO, :] = halo_ref[...]

    x = x_ref[0]
    p = _dot(x, wp_ref[...])
    if has_p2:
        p = p * _dot(x, wp2_ref[...])
    buf[HALO:HALO + tm, :] = p
    acc = p * cw_ref[K - 1:K, :]
    for k in range(K - 1):
        shift = K - 1 - k
        acc = acc + buf[pl.ds(HALO - shift, tm), :] * cw_ref[k:k + 1, :]
    if has_bias:
        acc = acc + cb_ref[...]
    if act:
        acc = _silu(acc)
    if has_gate:
        acc = acc * _dot(x, wg_ref[...])
    o_ref[0] = acc.astype(o_ref.dtype)
    last = buf[tm:tm + HALO, :]
    buf[0:HALO, :] = last
    tail_ref[...] = last


def _proj_conv(x, w, offs, conv_w, conv_b, halo, *, n_out, act, tm, tn, name):
    B, L, D = x.shape
    K = conv_w.shape[0]
    off_p, off_p2, off_g = offs
    has_p2, has_gate, has_bias = off_p2 is not None, off_g is not None, conv_b is not None

    def wspec(off):
        return pl.BlockSpec((D, tn), lambda j, b, i, o=off // tn: (0, j + o))

    colspec = lambda rows: pl.BlockSpec((rows, tn), lambda j, b, i: (0, j))
    args, specs = [x, w], [pl.BlockSpec((1, tm, D), lambda j, b, i: (b, i, 0)), wspec(off_p)]
    if has_p2:
        args.append(w); specs.append(wspec(off_p2))
    if has_gate:
        args.append(w); specs.append(wspec(off_g))
    args.append(conv_w); specs.append(colspec(K))
    if has_bias:
        args.append(conv_b); specs.append(colspec(1))
    args.append(halo); specs.append(colspec(HALO))
    body = functools.partial(_proj_conv_body, K=K, has_p2=has_p2, has_gate=has_gate,
                             has_bias=has_bias, act=act, tm=tm)
    return pl.pallas_call(
        body,
        grid=(n_out // tn, B, L // tm),
        in_specs=specs,
        out_specs=[pl.BlockSpec((1, tm, tn), lambda j, b, i: (b, i, j)), colspec(HALO)],
        out_shape=[jax.ShapeDtypeStruct((B, L, n_out), BF16),
                   jax.ShapeDtypeStruct((HALO, n_out), F32)],
        scratch_shapes=[pltpu.VMEM((tm + HALO, tn), F32)],
        compiler_params=_cparams(3),
        name=name,
    )(*args)


def _ssd_body(xs_ref, b_ref, c_ref, dt_ref, zs_ref, alog_ref, dexp_ref, g_ref, e2_ref, s0_ref,
              y_ref, sout_ref, S, *, Q, P, G, N, R):
    RP = R * P
    c = pl.program_id(1)

    @pl.when(c == 0)
    def _():
        S[...] = s0_ref[...]

    dt = dt_ref[0]
    dta = dt * (-jnp.exp(alog_ref[...]))
    rq = lax.broadcasted_iota(jnp.int32, (Q, Q), 0)
    cq = lax.broadcasted_iota(jnp.int32, (Q, Q), 1)
    causal = rq >= cq
    tril = jnp.where(causal, 1.0, 0.0).astype(BF16)
    acum = sum(_dot(tril, piece) for piece in _split3(dta))
    a_last = acum[Q - 1:Q, :]
    out_decay = jnp.exp(acum)
    w_in = dt * jnp.exp(a_last - acum)
    chunk_decay = jnp.exp(a_last)

    rl = lax.broadcasted_iota(jnp.int32, (LANES, LANES), 0)
    cl = lax.broadcasted_iota(jnp.int32, (LANES, LANES), 1)
    eye = jnp.where(rl == cl, 1.0, 0.0).astype(BF16)
    acum_t = sum(_dot_nt(eye, piece) for piece in _split3(acum))
    dt_t = sum(_dot_nt(eye, piece) for piece in _split3(dt))

    stack = jnp.concatenate([w_in, out_decay, jnp.broadcast_to(chunk_decay, (HALO, LANES))], axis=0)
    s_hi = stack.astype(BF16)
    s_lo = (stack - s_hi.astype(F32)).astype(BF16)
    stack2 = jnp.concatenate([s_hi, s_lo], axis=1)

    lane = lax.broadcasted_iota(jnp.int32, (Q, 2 * P), 1)
    for g in range(G):
        cs = slice(g * RP, (g + 1) * RP)
        ex = _dot(stack2, e2_ref[:, cs])
        w_exp, od_exp, cd_exp = ex[0:Q], ex[Q:2 * Q], ex[2 * Q:2 * Q + 1]
        xg = xs_ref[0, :, cs]
        xg32 = xg.astype(F32)
        bg = b_ref[0, :, g * N:(g + 1) * N]
        cg = c_ref[0, :, g * N:(g + 1) * N]
        cb = _dot_nt(cg, bg)
        sg = S[g]
        y_off = _dot(cg, sg.astype(BF16))
        S[g] = sg * cd_exp + _dot_tn(bg, (xg32 * w_exp).astype(BF16))
        ys = []
        for pr in range(R // 2):
            ms = []
            for h in (g * R + 2 * pr, g * R + 2 * pr + 1):
                seg = acum[:, h:h + 1] - acum_t[h:h + 1, :]
                dec = jnp.exp(jnp.where(causal, seg, -jnp.inf))
                ms.append((dec * cb * dt_t[h:h + 1, :]).astype(BF16))
            xp = xg[:, 2 * pr * P:(2 * pr + 2) * P]
            zero = jnp.zeros_like(xp)
            rhs = jnp.concatenate([jnp.where(lane < P, xp, zero), jnp.where(lane >= P, xp, zero)], axis=0)
            ys.append(_dot(jnp.concatenate(ms, axis=1), rhs))
        y = jnp.concatenate(ys, axis=1) + y_off * od_exp
        ym = (y + xg32 * dexp_ref[:, cs]) * zs_ref[0, :, cs].astype(F32)
        var = jnp.mean(ym * ym, axis=-1, keepdims=True)
        y_ref[0, :, cs] = (ym * lax.rsqrt(var + EPS) * g_ref[:, cs]).astype(y_ref.dtype)

    @pl.when(c == pl.num_programs(1) - 1)
    def _():
        sout_ref[...] = S[...]


def _ssd(xbc, dt, zs, a_log, d_exp, norm_g, e2, state0, *, DI, G, N, P):
    B, L, _ = xbc.shape
    Q = SSD_CHUNK
    H = DI // P
    R = H // G
    RP = R * P
    assert 2 * P == LANES and H <= LANES and R % 2 == 0 and L % Q == 0
    nb_x = DI // (G * N)
    body = functools.partial(_ssd_body, Q=Q, P=P, G=G, N=N, R=R)
    full2 = lambda shape: pl.BlockSpec(shape, lambda b, c: (0,) * len(shape))
    return pl.pallas_call(
        body,
        grid=(B, L // Q),
        in_specs=[pl.BlockSpec((1, Q, DI), lambda b, c: (b, c, 0)),
                  pl.BlockSpec((1, Q, G * N), lambda b, c: (b, c, nb_x)),
                  pl.BlockSpec((1, Q, G * N), lambda b, c: (b, c, nb_x + 1)),
                  pl.BlockSpec((1, Q, LANES), lambda b, c: (b, c, 0)),
                  pl.BlockSpec((1, Q, DI), lambda b, c: (b, c, 0)),
                  full2((1, LANES)), full2((1, DI)), full2((1, DI)), full2((2 * LANES, DI)),
                  full2((G, N, RP))],
        out_specs=[pl.BlockSpec((1, Q, DI), lambda b, c: (b, c, 0)), full2((G, N, RP))],
        out_shape=[jax.ShapeDtypeStruct((B, L, DI), BF16),
                   jax.ShapeDtypeStruct((G, N, RP), F32)],
        scratch_shapes=[pltpu.VMEM((G, N, RP), F32)],
        compiler_params=_cparams(2),
        name="ssd",
    )(xbc, xbc, xbc, dt, zs, a_log, d_exp, norm_g, e2, state0)


def _mix_body(ya_ref, ym_ref, ga_ref, gm_ref, wa_ref, wm_ref, o_ref):
    a = _dot(ya_ref[0], wa_ref[...])
    m = _dot(ym_ref[0], wm_ref[...])
    o_ref[0] = (ga_ref[0].astype(F32) * a + gm_ref[0].astype(F32) * m).astype(o_ref.dtype)


def _mix(ya, ym, gates, w_a, w_m, *, tm, tn):
    B, L, DA = ya.shape
    DM = ym.shape[2]
    D = w_a.shape[1]
    return pl.pallas_call(
        _mix_body,
        grid=(D // tn, B, L // tm),
        in_specs=[pl.BlockSpec((1, tm, DA), lambda j, b, i: (b, i, 0)),
                  pl.BlockSpec((1, tm, DM), lambda j, b, i: (b, i, 0)),
                  pl.BlockSpec((1, tm, tn), lambda j, b, i: (b, i, j)),
                  pl.BlockSpec((1, tm, tn), lambda j, b, i, o=D // tn: (b, i, j + o)),
                  pl.BlockSpec((DA, tn), lambda j, b, i: (0, j)),
                  pl.BlockSpec((DM, tn), lambda j, b, i: (0, j))],
        out_specs=pl.BlockSpec((1, tm, tn), lambda j, b, i: (b, i, j)),
        out_shape=jax.ShapeDtypeStruct((B, L, D), BF16),
        compiler_params=_cparams(3),
        name="gated_merge",
    )(ya, ym, gates, gates, w_a, w_m)


def _oproj_body(mix_ref, w_ref, h_ref, g_ref, h1_ref, xn_ref, *, n_pad, tm):
    h1 = h_ref[0] + _dot(mix_ref[0], w_ref[...])
    if n_pad:
        row = pl.program_id(1) * tm + lax.broadcasted_iota(jnp.int32, h1.shape, 0)
        h1 = jnp.where(row >= n_pad, h1, 0.0)
    h1_ref[0] = h1
    ms = jnp.mean(h1 * h1, axis=-1, keepdims=True)
    xn_ref[0] = (h1 * lax.rsqrt(ms + EPS) * g_ref[...]).astype(xn_ref.dtype)


def _oproj_norm(mix, w_o, h, g, *, tm, n_pad=0):
    B, L, D = h.shape
    body = functools.partial(_oproj_body, n_pad=n_pad, tm=tm)
    row = lambda: pl.BlockSpec((1, tm, D), lambda b, i: (b, i, 0))
    return pl.pallas_call(
        body,
        grid=(B, L // tm),
        in_specs=[row(), pl.BlockSpec((D, D), lambda b, i: (0, 0)), row(),
                  pl.BlockSpec((1, D), lambda b, i: (0, 0))],
        out_specs=[row(), row()],
        out_shape=[jax.ShapeDtypeStruct((B, L, D), F32), jax.ShapeDtypeStruct((B, L, D), BF16)],
        compiler_params=_cparams(2),
        name="oproj_norm",
    )(mix, w_o, h, g)


def _down_body(a_ref, w_ref, h_ref, g_ref, o_ref, acc):
    k = pl.program_id(2)

    @pl.when(k == 0)
    def _():
        acc[...] = h_ref[0]

    acc[...] += _dot(a_ref[0], w_ref[...])

    @pl.when(k == pl.num_programs(2) - 1)
    def _():
        h2 = acc[...]
        ms = jnp.mean(h2 * h2, axis=-1, keepdims=True)
        o_ref[0] = h2 * lax.rsqrt(ms + EPS) * g_ref[...]


def _down_norm(act, w_down, h1, g, *, tm, tk):
    B, L, D = h1.shape
    FP = act.shape[2]
    return pl.pallas_call(
        _down_body,
        grid=(B, L // tm, FP // tk),
        in_specs=[pl.BlockSpec((1, tm, tk), lambda b, i, k: (b, i, k)),
                  pl.BlockSpec((tk, D), lambda b, i, k: (k, 0)),
                  pl.BlockSpec((1, tm, D), lambda b, i, k: (b, i, 0)),
                  pl.BlockSpec((1, D), lambda b, i, k: (0, 0))],
        out_specs=pl.BlockSpec((1, tm, D), lambda b, i, k: (b, i, 0)),
        out_shape=jax.ShapeDtypeStruct((B, L, D), F32),
        scratch_shapes=[pltpu.VMEM((tm, D), F32)],
        compiler_params=_cparams(3),
        name="down_norm",
    )(act, w_down, h1, g)


def _block_pass(h, prm, halos, state0, *, n_pad, final):
    B, L, D = h.shape
    tm = min(L, 1024)
    d = prm["dims"]
    xn = _rmsnorm_cast(h, prm["norm1_g"], min(L, 512))
    ya, tail_sc = _proj_conv(xn, prm["w_sc"], (d["DSC"], 2 * d["DSC"], 0), prm["sc_conv_w"], None,
                             halos["sc"], n_out=d["DSC"], act=False, tm=tm,
                             tn=_tile(d["DSC"], 512), name="shortconv_proj")
    zs = _proj_pointwise(xn, prm["w_z"], prm["zero_z"], act="silu", out_dtype=BF16, tm=tm,
                         tn=_tile(d["DI"], 1024), name="z_proj")
    xbc, tail_xbc = _proj_conv(xn, prm["w_xbc"], (0, None, None), prm["mb_conv_w"], prm["mb_conv_b"],
                               halos["xbc"], n_out=d["DXBC"], act=True, tm=tm,
                               tn=_tile(d["DXBC"], 1024), name="xbc_proj")
    dt = _proj_pointwise(xn, prm["w_dt"], prm["dt_bias"], act="softplus", out_dtype=F32, tm=tm,
                         tn=LANES, n_pad=n_pad, name="dt_proj")
    gates = _proj_pointwise(xn, prm["w_gate"], prm["b_gate"], act="sigmoid", out_dtype=BF16, tm=tm,
                            tn=_tile(2 * D, 1024), name="gate_proj")
    ym, state = _ssd(xbc, dt, zs, prm["a_log"], prm["d_exp"], prm["mb_norm_g"], prm["e2"], state0,
                     DI=d["DI"], G=d["G"], N=SSD_STATE, P=d["P"])
    mix = _mix(ya, ym, gates, prm["w_a"], prm["w_m"], tm=min(L, 512), tn=_tile(D, 512))
    h1, xn2 = _oproj_norm(mix, prm["w_o"], h, prm["norm2_g"], tm=min(L, 256), n_pad=n_pad)
    act, tail_ffn = _proj_conv(xn2, prm["w_uv"], (0, None, d["FP"]), prm["ffn_conv_w"], prm["ffn_conv_b"],
                               halos["ffn"], n_out=d["FP"], act=True, tm=tm,
                               tn=_tile(d["FP"], 512), name="ffn_up")
    tails = {"sc": tail_sc, "xbc": tail_xbc, "ffn": tail_ffn}
    if not final:
        return None, tails, state
    out = _down_norm(act, prm["w_down"], h1, prm["normf_g"], tm=min(L, 512), tk=_tile(d["FP"], 1408))
    return out, tails, state


def kernel(x, meta_tokens, norm1_g, w_in, b_gate, sc_conv_w, mb_conv_w, mb_conv_b, dt_bias, a_log,
           d_skip, mb_norm_g, w_a, w_m, w_o, norm2_g, w_up, ffn_conv_w, ffn_conv_b, w_down, normf_g):
    assert norm1_g.shape[0] == 1, "single-layer block"
    B, L, D = x.shape
    n_meta = meta_tokens.shape[0]
    DSC = sc_conv_w.shape[-1]
    DI = w_m.shape[1]
    DXBC = mb_conv_w.shape[-1]
    H = dt_bias.shape[-1]
    P = DI // H
    G = (DXBC - DI) // (2 * SSD_STATE)
    F = ffn_conv_w.shape[-1]
    FP = -(-F // 512) * 512
    assert n_meta <= SSD_CHUNK and L % SSD_CHUNK == 0

    w = w_in[0]
    o_z = 3 * DSC
    o_xbc = o_z + DI
    o_dt = o_xbc + DXBC
    o_gate = o_dt + H
    row = lambda v: v.reshape(1, -1).astype(F32)
    padc = lambda a, n: jnp.pad(a, ((0, 0), (0, n - a.shape[1])))
    head_of_col = jnp.arange(DI) // P
    e = (jnp.arange(LANES)[:, None] == head_of_col[None, :]).astype(BF16)
    prm = {
        "dims": {"DSC": DSC, "DI": DI, "DXBC": DXBC, "G": G, "P": P, "FP": FP},
        "norm1_g": norm1_g[0],
        "w_sc": w[:, :o_z].astype(BF16),
        "w_z": w[:, o_z:o_xbc].astype(BF16),
        "w_xbc": w[:, o_xbc:o_dt].astype(BF16),
        "w_dt": padc(w[:, o_dt:o_gate], LANES).astype(BF16),
        "w_gate": w[:, o_gate:].astype(BF16),
        "zero_z": jnp.zeros((1, DI), F32),
        "dt_bias": padc(row(dt_bias[0]), LANES),
        "b_gate": row(b_gate[0]),
        "sc_conv_w": sc_conv_w[0].astype(F32),
        "mb_conv_w": mb_conv_w[0].astype(F32),
        "mb_conv_b": row(mb_conv_b[0]),
        "a_log": padc(row(a_log[0]), LANES),
        "d_exp": row(jnp.repeat(d_skip[0], P)),
        "mb_norm_g": row(mb_norm_g[0]),
        "e2": jnp.concatenate([e, e], axis=0),
        "w_a": w_a[0].astype(BF16),
        "w_m": w_m[0].astype(BF16),
        "w_o": w_o[0].astype(BF16),
        "norm2_g": row(norm2_g[0]),
        "w_uv": jnp.concatenate([padc(w_up[0][:, :F], FP), padc(w_up[0][:, F:], FP)], axis=1).astype(BF16),
        "ffn_conv_w": padc(ffn_conv_w[0].astype(F32), FP),
        "ffn_conv_b": padc(row(ffn_conv_b[0]), FP),
        "w_down": jnp.pad(w_down[0], ((0, FP - F), (0, 0))).astype(BF16),
        "normf_g": row(normf_g),
    }

    n_pad = SSD_CHUNK - n_meta
    hp = jnp.concatenate([jnp.zeros((n_pad, D), x.dtype), meta_tokens.astype(x.dtype)], axis=0)[None]
    zero_halos = {"sc": jnp.zeros((HALO, DSC), F32), "xbc": jnp.zeros((HALO, DXBC), F32),
                  "ffn": jnp.zeros((HALO, FP), F32)}
    state0 = jnp.zeros((G, SSD_STATE, DI // G), F32)
    _, halos, state = _block_pass(hp, prm, zero_halos, state0, n_pad=n_pad, final=False)

    out, _, _ = _block_pass(x, prm, halos, state, n_pad=0, final=True)
    return out
```
